```python
import math
import numpy as np
import jax
import jax.numpy as jnp
from jax import lax

D_MODEL = 2048
BATCH = 16
SEQ = 256
DEPTH = 4
DEC_BATCH = 4
DEC_SEQ = 1024
PAST_LEN = 512

GRID_W = 64
HEAD_DIM = 128
ROPE_THETA = 10000.0
NORM_EPS = 1e-6
MASK_VALUE = -1e30
LB_FLOOR = 1e-30
Q_BLOCK = 128
N_EVEN = (DEPTH + 1) // 2
N_ODD = DEPTH // 2
A_HEADS = 8
A_DK = 128
A_DV = 128
A_CHUNK = 32
B_HEADS = 8
B_KV = 2
C_HEADS = 8
C_KV = 2
WINDOW = 128
D_HEADS = 4
PEER_HEADS = 8
PEER_NKEYS = 128
PEER_EXPERTS = PEER_NKEYS * PEER_NKEYS
PEER_QDIM = 256
PEER_TOPK = 16
PEER_TOKEN_BLOCK = 128

EVEN_SPLITS = (A_HEADS * A_DK, A_HEADS * A_DK, A_HEADS * A_DK, A_HEADS * A_DV, A_HEADS * A_DV,
               B_HEADS * HEAD_DIM, B_KV * HEAD_DIM, B_KV * HEAD_DIM)
EVEN_IN = sum(EVEN_SPLITS)
EVEN_OUT = A_HEADS * A_DV + B_HEADS * HEAD_DIM
ODD_SPLITS = (C_HEADS * HEAD_DIM, C_KV * HEAD_DIM, C_KV * HEAD_DIM,
              D_HEADS * 2 * HEAD_DIM, D_HEADS * 2 * HEAD_DIM, D_HEADS * 2 * HEAD_DIM)
ODD_IN = sum(ODD_SPLITS)
ODD_OUT = C_HEADS * HEAD_DIM + D_HEADS * 2 * HEAD_DIM

kernel_name = 'hybrid_diffusion_prefix_trunk'


def rmsnorm(x, g):
    x32 = x.astype(jnp.float32)
    y = x32 * lax.rsqrt(jnp.mean(x32 * x32, axis=-1, keepdims=True) + NORM_EPS)
    return (y * g.astype(jnp.float32)).astype(x.dtype)


def split_cols(x, sizes):
    offsets = [int(o) for o in np.cumsum(sizes)[:-1]]
    return jnp.split(x, offsets, axis=-1)


def to_heads(x, n_heads):
    b, n, _ = x.shape
    return x.reshape(b, n, n_heads, -1).transpose(0, 2, 1, 3)


def merge_heads(x):
    b, h, n, d = x.shape
    return x.transpose(0, 2, 1, 3).reshape(b, n, h * d)


def axial_rope_angles(n_tokens):
    rows = n_tokens // GRID_W
    row_ids = jnp.repeat(jnp.arange(rows), GRID_W).astype(jnp.float32)
    col_ids = jnp.tile(jnp.arange(GRID_W), rows).astype(jnp.float32)
    half = HEAD_DIM // 2
    inv_freq = 1.0 / (ROPE_THETA ** (jnp.arange(0, half, 2, dtype=jnp.float32) / half))
    return row_ids[:, None] * inv_freq, col_ids[:, None] * inv_freq


def rope_axis(x, ang):
    x1, x2 = jnp.split(x, 2, axis=-1)
    cos, sin = jnp.cos(ang), jnp.sin(ang)
    return jnp.concatenate([x1 * cos - x2 * sin, x2 * cos + x1 * sin], axis=-1)


def rope_2d(x, ang_row, ang_col):
    half = HEAD_DIM // 2
    x32 = x.astype(jnp.float32)
    out = jnp.concatenate([rope_axis(x32[..., :half], ang_row), rope_axis(x32[..., half:], ang_col)], axis=-1)
    return out.astype(x.dtype)


def blocked_attention(q, k, v, sink=None):
    b, h, n, dh = q.shape
    kv = k.shape[1]
    g = h // kv
    dv = v.shape[-1]
    nb = n // Q_BLOCK
    qb = jnp.moveaxis((q * (dh ** -0.5)).reshape(b, kv, g, nb, Q_BLOCK, dh), 3, 0)

    def one_block(qi):
        s = jnp.einsum('bkgqd,bkmd->bkgqm', qi, k).astype(jnp.float32)
        if sink is not None:
            sk = jnp.broadcast_to(sink.astype(jnp.float32).reshape(1, kv, g, 1, 1), s.shape[:-1] + (1,))
            p = jax.nn.softmax(jnp.concatenate([s, sk], axis=-1), axis=-1)[..., :-1]
        else:
            p = jax.nn.softmax(s, axis=-1)
        return jnp.einsum('bkgqm,bkmd->bkgqd', p.astype(v.dtype), v)

    o = lax.map(one_block, qb)
    return jnp.moveaxis(o, 0, 3).reshape(b, h, n, dv)


def window_attention(q, k, v, k_ctx, v_ctx, sink):
    b, h, n, dh = q.shape
    kv = k.shape[1]
    g = h // kv
    dv = v.shape[-1]
    nb = n // Q_BLOCK
    span = Q_BLOCK + 2 * WINDOW
    pad = ((0, 0), (0, 0), (WINDOW, WINDOW), (0, 0))
    key_idx = jnp.arange(nb)[:, None] * Q_BLOCK + jnp.arange(span)[None, :]
    kb = jnp.pad(k, pad)[:, :, key_idx]
    vb = jnp.pad(v, pad)[:, :, key_idx]
    q_pos = jnp.arange(nb)[:, None] * Q_BLOCK + jnp.arange(Q_BLOCK)[None, :]
    k_pos = key_idx - WINDOW
    valid = ((k_pos[:, None, :] >= 0) & (k_pos[:, None, :] < n)
             & (jnp.abs(q_pos[:, :, None] - k_pos[:, None, :]) <= WINDOW))
    qb = (q * (dh ** -0.5)).reshape(b, kv, g, nb, Q_BLOCK, dh)
    s_loc = jnp.where(valid, jnp.einsum('bkgnqd,bknsd->bkgnqs', qb, kb).astype(jnp.float32), MASK_VALUE)
    s_ctx = jnp.einsum('bkgnqd,bkmd->bkgnqm', qb, k_ctx).astype(jnp.float32)
    sk = jnp.broadcast_to(sink.astype(jnp.float32).reshape(1, kv, g, 1, 1, 1), s_loc.shape[:-1] + (1,))
    p = jax.nn.softmax(jnp.concatenate([s_loc, s_ctx, sk], axis=-1), axis=-1)
    p_loc = p[..., :span].astype(v.dtype)
    p_ctx = p[..., span:span + k_ctx.shape[2]].astype(v.dtype)
    o = (jnp.einsum('bkgnqs,bknsd->bkgnqd', p_loc, vb)
         + jnp.einsum('bkgnqm,bkmd->bkgnqd', p_ctx, v_ctx))
    return o.reshape(b, h, n, dv)


def gla_chunk_scan(q, k, v, log_f, s0):
    b, h, n, _ = q.shape
    dv = v.shape[-1]
    nc = n // A_CHUNK

    def chunks(t):
        return jnp.moveaxis(t.astype(jnp.float32).reshape(b, h, nc, A_CHUNK, t.shape[-1]), 2, 0)

    incl = jnp.tril(jnp.ones((A_CHUNK, A_CHUNK), dtype=bool))[:, :, None]

    def step(state, inp):
        qc, kc, vc, lf = inp
        bc = jnp.cumsum(lf, axis=2)
        btot = bc[:, :, -1]
        o_inter = jnp.einsum('bhtk,bhkv->bhtv', qc * jnp.exp(bc), state)
        diff = bc[:, :, :, None, :] - bc[:, :, None, :, :]
        decay = jnp.where(incl, jnp.exp(jnp.where(incl, diff, 0.0)), 0.0)
        scores = jnp.einsum('bhtk,bhsk,bhtsk->bhts', qc, kc, decay)
        o_intra = jnp.einsum('bhts,bhsv->bhtv', scores, vc)
        state = (jnp.exp(btot)[..., None] * state
                 + jnp.einsum('bhsk,bhsv->bhkv', kc * jnp.exp(btot[:, :, None, :] - bc), vc))
        return state, o_inter + o_intra

    s_fin, o = lax.scan(step, s0.astype(jnp.float32), (chunks(q), chunks(k), chunks(v), chunks(log_f)))
    return jnp.moveaxis(o, 0, 2).reshape(b, h, n, dv), s_fin


def hgrn2_mix(q, z_fwd, z_bwd, i, g, lb_fwd, lb_bwd, s0_fwd, s0_bwd, norm_g):
    qh = to_heads(q, A_HEADS)
    ih = to_heads(i, A_HEADS)

    def direction(z, lb, s0, reverse):
        z = to_heads(z, A_HEADS).astype(jnp.float32)
        lb = lb.reshape(A_HEADS, 1, A_DK)
        log_f = jnp.logaddexp(jnp.log(jnp.maximum(lb, LB_FLOOR)), jnp.log1p(-lb) + jax.nn.log_sigmoid(z))
        k = (1.0 - lb) * jax.nn.sigmoid(-z)
        qq, vv = qh, ih
        if reverse:
            qq, k, vv, log_f = (jnp.flip(t, axis=2) for t in (qq, k, vv, log_f))
        o, s = gla_chunk_scan(qq, k, vv, log_f, s0)
        if reverse:
            o = jnp.flip(o, axis=2)
        return o, s

    o_f, s_f = direction(z_fwd, lb_fwd, s0_fwd, False)
    o_b, s_b = direction(z_bwd, lb_bwd, s0_bwd, True)
    o = rmsnorm(o_f + o_b, norm_g) * jax.nn.silu(to_heads(g, A_HEADS).astype(jnp.float32))
    return merge_heads(o).astype(q.dtype), s_f, s_b


def even_mix(h, w_in, w_out, lb_fwd, lb_bwd, hgrn_g, qn_g, kn_g, ctx, rope):
    b = h.shape[0]
    aq, azf, azb, ai, ag, bq, bk, bv = split_cols(h @ w_in, EVEN_SPLITS)
    if ctx is None:
        s0 = jnp.zeros((b, 2, A_HEADS, A_DK, A_DV), jnp.float32)
    else:
        s0 = ctx[0]
    o_a, s_f, s_b = hgrn2_mix(aq, azf, azb, ai, ag, lb_fwd, lb_bwd, s0[:, 0], s0[:, 1], hgrn_g)
    q = rmsnorm(to_heads(bq, B_HEADS), qn_g)
    k = rmsnorm(to_heads(bk, B_KV), kn_g)
    v = to_heads(bv, B_KV)
    if ctx is None:
        o_b = blocked_attention(q, k, v)
        new = (jnp.stack([s_f, s_b], axis=1), k, v)
    else:
        q = rope_2d(q, *rope)
        k = rope_2d(k, *rope)
        o_b = blocked_attention(q, jnp.concatenate([k, ctx[1]], axis=2), jnp.concatenate([v, ctx[2]], axis=2))
        new = None
    out = jnp.concatenate([o_a, merge_heads(o_b)], axis=-1) @ w_out
    return out, new


def odd_mix(h, w_in, w_out, sink, lam_vecs, subln_g, lam_init, ctx, rope):
    b, n, _ = h.shape
    cq, ck, cv, dq, dk, dv = split_cols(h @ w_in, ODD_SPLITS)
    cq = to_heads(cq, C_HEADS)
    ck = to_heads(ck, C_KV)
    cv = to_heads(cv, C_KV)
    dq = dq.reshape(b, n, D_HEADS, 2, HEAD_DIM).transpose(0, 2, 3, 1, 4)
    dk = dk.reshape(b, n, D_HEADS, 2, HEAD_DIM).transpose(0, 2, 3, 1, 4)
    dv = to_heads(dv, D_HEADS)
    lv = lam_vecs.astype(jnp.float32)
    lam = jnp.exp(jnp.sum(lv[0] * lv[1])) - jnp.exp(jnp.sum(lv[2] * lv[3])) + lam_init
    if ctx is None:
        o_c = blocked_attention(cq, ck, cv, sink)
        k_all, v_all = dk, dv
        new = (ck, cv, dk, dv)
    else:
        cq = rope_2d(cq, *rope)
        ck = rope_2d(ck, *rope)
        o_c = window_attention(cq, ck, cv, ctx[0], ctx[1], sink)
        dq = rope_2d(dq, *rope)
        k_all = jnp.concatenate([rope_2d(dk, *rope), ctx[2]], axis=3)
        v_all = jnp.concatenate([dv, ctx[3]], axis=2)
        new = None
    o1 = blocked_attention(dq[:, :, 0], k_all[:, :, 0], v_all)
    o2 = blocked_attention(dq[:, :, 1], k_all[:, :, 1], v_all)
    o_d = rmsnorm(o1.astype(jnp.float32) - lam * o2.astype(jnp.float32), subln_g) * (1.0 - lam_init)
    out = jnp.concatenate([merge_heads(o_c), merge_heads(o_d).astype(h.dtype)], axis=-1) @ w_out
    return out, new


def peer_ffn(h, wq, subkeys, u_tab, v_tab):
    b, n, d = h.shape
    t = b * n
    hf = h.reshape(t, d)
    q = (hf @ wq).reshape(t, PEER_HEADS, 2, PEER_QDIM // 2)
    s = jnp.einsum('thpc,hpkc->thpk', q, subkeys).astype(jnp.float32)
    s1, i1 = lax.top_k(s[:, :, 0], PEER_TOPK)
    s2, i2 = lax.top_k(s[:, :, 1], PEER_TOPK)
    cand_s = (s1[..., :, None] + s2[..., None, :]).reshape(t, PEER_HEADS, PEER_TOPK * PEER_TOPK)
    cand_e = (i1[..., :, None] * PEER_NKEYS + i2[..., None, :]).reshape(t, PEER_HEADS, PEER_TOPK * PEER_TOPK)
    top_s, pos = lax.top_k(cand_s, PEER_TOPK)
    experts = jnp.take_along_axis(cand_e, pos, axis=-1).reshape(t, PEER_HEADS * PEER_TOPK)
    gates = jax.nn.softmax(top_s, axis=-1).reshape(t, PEER_HEADS * PEER_TOPK)
    nb = t // PEER_TOKEN_BLOCK

    def one_block(args):
        hb, eb, gb = args
        act = jax.nn.gelu(jnp.einsum('tkd,td->tk', u_tab[eb], hb).astype(jnp.float32), approximate=False)
        return jnp.einsum('tk,tkd->td', (gb * act).astype(v_tab.dtype), v_tab[eb])

    out = lax.map(one_block, (hf.reshape(nb, PEER_TOKEN_BLOCK, d),
                              experts.reshape(nb, PEER_TOKEN_BLOCK, -1),
                              gates.reshape(nb, PEER_TOKEN_BLOCK, -1)))
    return out.reshape(b, n, d).astype(h.dtype)


def adaln(cond, w, bias):
    m = (jax.nn.silu(cond) @ w + bias).reshape(cond.shape[0], 6, D_MODEL)
    return [m[:, i][:, None, :] for i in range(6)]


def setup_inputs(seed: int = 0) -> dict:
    key = jax.random.key(seed)
    keys = jax.random.split(key, 32)

    def nrm(i, shape, scale):
        return jax.random.normal(keys[i], shape, jnp.float32) * scale

    D = D_MODEL
    return {
        'x_prompt': nrm(0, (BATCH, SEQ, D), 1.0),
        'x_sample': nrm(1, (DEC_BATCH, DEC_SEQ, D), 1.0),
        'state_hgrn': nrm(2, (DEC_BATCH, N_EVEN, 2, A_HEADS, A_DK, A_DV), 0.5),
        'cache_b_k': nrm(3, (DEC_BATCH, N_EVEN, B_KV, PAST_LEN, HEAD_DIM), 1.0),
        'cache_b_v': nrm(4, (DEC_BATCH, N_EVEN, B_KV, PAST_LEN, HEAD_DIM), 1.0),
        'cache_c_k': nrm(5, (DEC_BATCH, N_ODD, C_KV, PAST_LEN, HEAD_DIM), 1.0),
        'cache_c_v': nrm(6, (DEC_BATCH, N_ODD, C_KV, PAST_LEN, HEAD_DIM), 1.0),
        'cache_d_k': nrm(7, (DEC_BATCH, N_ODD, D_HEADS, 2, PAST_LEN, HEAD_DIM), 1.0),
        'cache_d_v': nrm(8, (DEC_BATCH, N_ODD, D_HEADS, PAST_LEN, 2 * HEAD_DIM), 1.0),
        'c': nrm(9, (DEC_BATCH, D), 1.0),
        'c_ctx': nrm(10, (D,), 1.0),
        'norm_g': 1.0 + nrm(11, (DEPTH, 2, D), 0.02),
        'w_ada': nrm(12, (DEPTH, D, 6 * D), 0.5 * D ** -0.5),
        'b_ada': nrm(13, (DEPTH, 6 * D), 0.02),
        'w_in_even': nrm(14, (N_EVEN, D, EVEN_IN), D ** -0.5),
        'w_out_even': nrm(15, (N_EVEN, EVEN_OUT, D), EVEN_OUT ** -0.5),
        'hgrn_lb_logits': nrm(16, (2, N_EVEN, A_HEADS * A_DK), 1.0),
        'hgrn_norm_g': 1.0 + nrm(17, (N_EVEN, A_DV), 0.02),
        'b_qnorm_g': 1.0 + nrm(18, (N_EVEN, HEAD_DIM), 0.02),
        'b_knorm_g': 1.0 + nrm(19, (N_EVEN, HEAD_DIM), 0.02),
        'w_in_odd': nrm(20, (N_ODD, D, ODD_IN), D ** -0.5),
        'w_out_odd': nrm(21, (N_ODD, ODD_OUT, D), ODD_OUT ** -0.5),
        'c_sink': nrm(22, (N_ODD, C_HEADS), 1.0),
        'd_lambda': nrm(23, (N_ODD, 4, HEAD_DIM), 0.1),
        'd_subln_g': 1.0 + nrm(24, (N_ODD, 2 * HEAD_DIM), 0.02),
        'peer_wq': nrm(25, (DEPTH, D, PEER_HEADS * PEER_QDIM), D ** -0.5),
        'peer_subkeys': nrm(26, (DEPTH, PEER_HEADS, 2, PEER_NKEYS, PEER_QDIM // 2), (PEER_QDIM // 2) ** -0.5),
        'peer_u': nrm(27, (DEPTH, PEER_EXPERTS, D), D ** -0.5),
        'peer_v': nrm(28, (DEPTH, PEER_EXPERTS, D), 0.5),
        'final_norm_g': 1.0 + nrm(29, (D,), 0.02),
    }


def reference(x_prompt, x_sample, state_hgrn, cache_b_k, cache_b_v, cache_c_k, cache_c_v, cache_d_k, cache_d_v,
              c, c_ctx, norm_g, w_ada, b_ada, w_in_even, w_out_even, hgrn_lb_logits, hgrn_norm_g, b_qnorm_g,
              b_knorm_g, w_in_odd, w_out_odd, c_sink, d_lambda, d_subln_g, peer_wq, peer_subkeys, peer_u, peer_v,
              final_norm_g):
    lb_p = jax.nn.softmax(hgrn_lb_logits.astype(jnp.float32), axis=1)
    lower_bounds = jnp.cumsum(lb_p, axis=1) - lb_p[:, :1]

    def trunk_layer(x, l, cond, ctx, rope):
        sh1, sc1, g1, sh2, sc2, g2 = adaln(cond, w_ada[l], b_ada[l])
        h = rmsnorm(x, norm_g[l, 0]) * (1.0 + sc1) + sh1
        j = l // 2
        if l % 2 == 0:
            out, new = even_mix(h, w_in_even[j], w_out_even[j], lower_bounds[0, j], lower_bounds[1, j],
                                hgrn_norm_g[j], b_qnorm_g[j], b_knorm_g[j], ctx, rope)
        else:
            lam_init = 0.8 - 0.6 * math.exp(-0.3 * l)
            out, new = odd_mix(h, w_in_odd[j], w_out_odd[j], c_sink[j], d_lambda[j], d_subln_g[j],
                               lam_init, ctx, rope)
        x = x + g1 * out
        h = rmsnorm(x, norm_g[l, 1]) * (1.0 + sc2) + sh2
        x = x + g2 * peer_ffn(h, peer_wq[l], peer_subkeys[l], peer_u[l], peer_v[l])
        return x, new

    xp = x_prompt
    even_new = []
    odd_new = []
    cond_ctx = c_ctx[None, :]
    for l in range(DEPTH):
        xp, new = trunk_layer(xp, l, cond_ctx, None, None)
        if l % 2 == 0:
            even_new.append(new)
        else:
            odd_new.append(new)
    y_prompt = rmsnorm(xp, final_norm_g)
    new_state_hgrn = jnp.stack([t[0] for t in even_new], axis=1)
    new_cache_b_k = jnp.stack([t[1] for t in even_new], axis=1)
    new_cache_b_v = jnp.stack([t[2] for t in even_new], axis=1)
    new_cache_c_k = jnp.stack([t[0] for t in odd_new], axis=1)
    new_cache_c_v = jnp.stack([t[1] for t in odd_new], axis=1)
    new_cache_d_k = jnp.stack([t[2] for t in odd_new], axis=1)
    new_cache_d_v = jnp.stack([t[3] for t in odd_new], axis=1)

    rope = axial_rope_angles(x_sample.shape[1])
    xs = x_sample
    for l in range(DEPTH):
        j = l // 2
        if l % 2 == 0:
            ctx = (state_hgrn[:, j], cache_b_k[:, j], cache_b_v[:, j])
        else:
            ctx = (cache_c_k[:, j], cache_c_v[:, j], cache_d_k[:, j], cache_d_v[:, j])
        xs, _ = trunk_layer(xs, l, c, ctx, rope)
    y_sample = rmsnorm(xs, final_norm_g)

    return (y_prompt, y_sample, new_state_hgrn, new_cache_b_k, new_cache_b_v, new_cache_c_k, new_cache_c_v,
            new_cache_d_k, new_cache_d_v)
```

```python
import functools
import math

import numpy as np
import jax
import jax.numpy as jnp
from jax import lax
from jax.experimental import pallas as pl
from jax.experimental.pallas import tpu as pltpu

F32 = jnp.float32
BF16 = jnp.bfloat16

D_MODEL = 2048
N_CTX_B, N_CTX = 16, 256
N_LAT_B, N_LAT = 4, 1024
PAST = 512
DEPTH = 4
T_CTX = N_CTX_B * N_CTX
T_LAT = N_LAT_B * N_LAT
T_ALL = T_CTX + T_LAT
HD = 128
GRID_W = 64
ROPE_THETA = 10000.0
EPS = 1e-6
MASK_VALUE = -1e30
LB_FLOOR = 1e-30
WINDOW = 128
A_HEADS = 8
A_CHUNK = 32
B_HEADS, B_KV = 8, 2
C_HEADS, C_KV = 8, 2
D_HEADS = 4
P_HEADS = 8
P_KEYS = 128
P_EXPERTS = P_KEYS * P_KEYS
P_TOPK = 16
EVEN_IN = 7680
ODD_IN = 6144
N_MOD_ROWS = 8

LANE = 128
VMEM_LIMIT = 56 * 1024 * 1024
ROW_TILE = 1024
PEER_TT = 512
PEER_EB = 512


def _cparams(sem):
    return pltpu.CompilerParams(dimension_semantics=sem, vmem_limit_bytes=VMEM_LIMIT)


def _mod_row(i, tm):
    n_ctx_tiles = T_CTX // tm
    per_batch = N_LAT // tm
    return jnp.where(i < n_ctx_tiles, 0, 1 + (i - n_ctx_tiles) // per_batch)


def _mod_spec(l, chunk, tm):
    return pl.BlockSpec((None, None, 1, D_MODEL), lambda i, *_: (l, _mod_row(i, tm), 0, chunk))


def _rms(x, g):
    ms = jnp.mean(x * x, axis=-1, keepdims=True)
    return x * lax.rsqrt(ms + EPS) * g


def _ada_kernel(c_ref, w_ref, b_ref, o_ref):
    c = c_ref[...]
    s = (c * jax.nn.sigmoid(c)).astype(BF16)
    o_ref[...] = jnp.dot(s, w_ref[...].astype(BF16), preferred_element_type=F32) + b_ref[...]


def _ada(cond, w_ada, b_ada):
    tn = 1024
    n = 6 * D_MODEL
    return pl.pallas_call(
        _ada_kernel,
        grid=(DEPTH, n // tn),
        in_specs=[
            pl.BlockSpec((N_MOD_ROWS, D_MODEL), lambda l, j: (0, 0)),
            pl.BlockSpec((None, D_MODEL, tn), lambda l, j: (l, 0, j)),
            pl.BlockSpec((None, 1, tn), lambda l, j: (l, 0, j)),
        ],
        out_specs=pl.BlockSpec((None, N_MOD_ROWS, tn), lambda l, j: (l, 0, j)),
        out_shape=jax.ShapeDtypeStruct((DEPTH, N_MOD_ROWS, n), F32),
        compiler_params=_cparams(("arbitrary", "arbitrary")),
        name="ada",
    )(cond, w_ada, b_ada.reshape(DEPTH, 1, n))


def _pre_in_kernel(x_ref, g_ref, sc_ref, sh_ref, w_ref, o_ref, h_ref):
    @pl.when(pl.program_id(1) == 0)
    def _():
        y = _rms(x_ref[...], g_ref[...])
        h_ref[...] = (y * (1.0 + sc_ref[...]) + sh_ref[...]).astype(BF16)

    o_ref[...] = jnp.dot(h_ref[...], w_ref[...], preferred_element_type=F32)


def _pre_in(x, g, m4, l, w):
    tm, tn = ROW_TILE, 512
    n = w.shape[1]
    return pl.pallas_call(
        _pre_in_kernel,
        grid=(T_ALL // tm, n // tn),
        in_specs=[
            pl.BlockSpec((tm, D_MODEL), lambda i, j: (i, 0)),
            pl.BlockSpec((1, D_MODEL), lambda i, j: (0, 0)),
            _mod_spec(l, 1, tm),
            _mod_spec(l, 0, tm),
            pl.BlockSpec((D_MODEL, tn), lambda i, j: (0, j)),
        ],
        out_specs=pl.BlockSpec((tm, tn), lambda i, j: (i, j)),
        out_shape=jax.ShapeDtypeStruct((T_ALL, n), F32),
        scratch_shapes=[pltpu.VMEM((tm, D_MODEL), BF16)],
        compiler_params=_cparams(("arbitrary", "arbitrary")),
        name="pre_in",
    )(x, g.reshape(1, D_MODEL), m4, m4, w)


def _post_out_kernel(ma_ref, mb_ref, w_ref, x_ref, g1_ref, ng_ref, sc_ref, sh_ref, x1_ref, ht_ref):
    half = D_MODEL // 2
    out = jnp.dot(ma_ref[...], w_ref[0:half, :], preferred_element_type=F32)
    out = out + jnp.dot(mb_ref[...], w_ref[half:, :], preferred_element_type=F32)
    x1 = x_ref[...] + g1_ref[...] * out
    x1_ref[...] = x1
    h2 = _rms(x1, ng_ref[...]) * (1.0 + sc_ref[...]) + sh_ref[...]
    ht_ref[...] = jnp.transpose(h2).astype(BF16)


def _post_out(mix_a, mix_b, w_out, x, m4, l, ng):
    tm = 256
    half = D_MODEL // 2
    return pl.pallas_call(
        _post_out_kernel,
        grid=(T_ALL // tm,),
        in_specs=[
            pl.BlockSpec((tm, half), lambda i: (i, 0)),
            pl.BlockSpec((tm, half), lambda i: (i, 0)),
            pl.BlockSpec((D_MODEL, D_MODEL), lambda i: (0, 0)),
            pl.BlockSpec((tm, D_MODEL), lambda i: (i, 0)),
            _mod_spec(l, 2, tm),
            pl.BlockSpec((1, D_MODEL), lambda i: (0, 0)),
            _mod_spec(l, 4, tm),
            _mod_spec(l, 3, tm),
        ],
        out_specs=[
            pl.BlockSpec((tm, D_MODEL), lambda i: (i, 0)),
            pl.BlockSpec((D_MODEL, tm), lambda i: (0, i)),
        ],
        out_shape=[
            jax.ShapeDtypeStruct((T_ALL, D_MODEL), F32),
            jax.ShapeDtypeStruct((D_MODEL, T_ALL), BF16),
        ],
        compiler_params=_cparams(("arbitrary",)),
        name="post_out",
    )(mix_a, mix_b, w_out, x, m4, ng.reshape(1, D_MODEL), m4, m4)


def _top_values(x, k):
    rows = x.shape[0]
    ridx = lax.broadcasted_iota(jnp.int32, x.shape, 0).astype(F32)
    vals = []
    for r in range(k):
        m = jnp.max(x, axis=0, keepdims=True)
        vals.append(m)
        if r + 1 < k:
            first = jnp.min(jnp.where(x == m, ridx, float(rows)), axis=0, keepdims=True)
            x = jnp.where(ridx == first, -jnp.inf, x)
    return vals


def _peer_scores_kernel(ht_ref, wqt_ref, sk_ref, s1_ref, s2_ref, e1_ref, e2_ref, tau_ref):
    ht = ht_ref[...]
    for h in range(P_HEADS):
        top = []
        for p in range(2):
            hp = 2 * h + p
            qt = jnp.dot(wqt_ref[hp * P_KEYS:(hp + 1) * P_KEYS, :], ht, preferred_element_type=F32)
            st = jnp.dot(sk_ref[hp], qt.astype(BF16), preferred_element_type=F32)
            (s1_ref if p == 0 else s2_ref)[h * P_KEYS:(h + 1) * P_KEYS, :] = st
            top.append(_top_values(st, P_TOPK))
        v2 = jnp.concatenate(top[1], axis=0)
        sums = jnp.concatenate([v1 + v2 for v1 in top[0]], axis=0)
        best = _top_values(sums, P_TOPK)
        z = jnp.ones_like(best[0])
        for r in range(1, P_TOPK):
            z = z + jnp.exp(best[r] - best[0])
        tau_ref[h:h + 1, :] = best[P_TOPK - 1]
        rows = slice(h * P_KEYS, (h + 1) * P_KEYS)
        e1_ref[rows, :] = jnp.exp(s1_ref[rows, :] - top[0][0]) / z
        e2_ref[rows, :] = jnp.exp(s2_ref[rows, :] - top[1][0])


def _peer_scores(ht, wqt, sk):
    tt = PEER_TT
    hk = P_HEADS * P_KEYS
    big = pl.BlockSpec((hk, tt), lambda t: (0, t))
    return pl.pallas_call(
        _peer_scores_kernel,
        grid=(T_ALL // tt,),
        in_specs=[
            pl.BlockSpec((D_MODEL, tt), lambda t: (0, t)),
            pl.BlockSpec((D_MODEL, D_MODEL), lambda t: (0, 0)),
            pl.BlockSpec((2 * P_HEADS, P_KEYS, P_KEYS), lambda t: (0, 0, 0)),
        ],
        out_specs=[big, big, big, big, pl.BlockSpec((P_HEADS, tt), lambda t: (0, t))],
        out_shape=[jax.ShapeDtypeStruct((hk, T_ALL), F32)] * 4 + [jax.ShapeDtypeStruct((P_HEADS, T_ALL), F32)],
        compiler_params=_cparams(("arbitrary",)),
        name="peer_scores",
    )(ht, wqt, sk)


def _peer_main_kernel(ht_ref, u_ref, vt_ref, s1_ref, s2_ref, e1_ref, e2_ref, tau_ref, o_ref, g_ref):
    e = pl.program_id(1)

    @pl.when(e == 0)
    def _():
        o_ref[...] = jnp.zeros_like(o_ref)

    act = jnp.dot(u_ref[...], ht_ref[...], preferred_element_type=F32)
    groups = PEER_EB // P_KEYS
    for il in range(groups):
        i = e * groups + il
        a = act[il * P_KEYS:(il + 1) * P_KEYS, :]
        gelu = 0.5 * a * (1.0 + lax.erf(a * (1.0 / math.sqrt(2.0))))
        gate = jnp.zeros_like(a)
        for h in range(P_HEADS):
            rows = slice(h * P_KEYS, (h + 1) * P_KEYS)
            s1 = s1_ref[pl.ds(h * P_KEYS + i, 1), :]
            w1 = e1_ref[pl.ds(h * P_KEYS + i, 1), :]
            hit = (s1 + s2_ref[rows, :]) >= tau_ref[h:h + 1, :]
            gate = gate + jnp.where(hit, e2_ref[rows, :], 0.0) * w1
        g_ref[il * P_KEYS:(il + 1) * P_KEYS, :] = (gate * gelu).astype(BF16)
    o_ref[...] += jnp.dot(vt_ref[...], g_ref[...], preferred_element_type=F32)


def _peer_main(ht, u, vt, s1, s2, e1, e2, tau):
    tt, eb = PEER_TT, PEER_EB
    hk = P_HEADS * P_KEYS
    big = pl.BlockSpec((hk, tt), lambda t, e: (0, t))
    return pl.pallas_call(
        _peer_main_kernel,
        grid=(T_ALL // tt, P_EXPERTS // eb),
        in_specs=[
            pl.BlockSpec((D_MODEL, tt), lambda t, e: (0, t)),
            pl.BlockSpec((eb, D_MODEL), lambda t, e: (e, 0)),
            pl.BlockSpec((D_MODEL, eb), lambda t, e: (0, e)),
            big, big, big, big,
            pl.BlockSpec((P_HEADS, tt), lambda t, e: (0, t)),
        ],
        out_specs=pl.BlockSpec((D_MODEL, tt), lambda t, e: (0, t)),
        out_shape=jax.ShapeDtypeStruct((D_MODEL, T_ALL), F32),
        scratch_shapes=[pltpu.VMEM((eb, tt), BF16)],
        compiler_params=_cparams(("arbitrary", "arbitrary")),
        name="peer_main",
    )(ht, u, vt, s1, s2, e1, e2, tau)


def _peer_post_kernel(ot_ref, x1_ref, g2_ref, fg_ref, x2_ref, y_ref=None):
    x2 = x1_ref[...] + g2_ref[...] * jnp.transpose(ot_ref[...])
    x2_ref[...] = x2
    if y_ref is not None:
        y_ref[...] = _rms(x2, fg_ref[...])


def _peer_post(ot, x1, m4, l, final_g):
    tm = 256
    last = final_g is not None
    row = pl.BlockSpec((tm, D_MODEL), lambda i: (i, 0))
    fg = final_g if last else jnp.ones((D_MODEL,), F32)
    outs = pl.pallas_call(
        _peer_post_kernel,
        grid=(T_ALL // tm,),
        in_specs=[
            pl.BlockSpec((D_MODEL, tm), lambda i: (0, i)),
            row,
            _mod_spec(l, 5, tm),
            pl.BlockSpec((1, D_MODEL), lambda i: (0, 0)),
        ],
        out_specs=[row, row] if last else [row],
        out_shape=[jax.ShapeDtypeStruct((T_ALL, D_MODEL), F32)] * (2 if last else 1),
        compiler_params=_cparams(("arbitrary",)),
        name="peer_post",
    )(ot, x1, m4, fg.reshape(1, D_MODEL))
    return outs if last else (outs[0], None)


def _hgrn_kernel(*refs, n, has_s0, want_state):
    q_ref, zf_ref, zb_ref, v_ref, g_ref, lb_ref, ng_ref = refs[:7]
    pos = 7
    s0_ref = None
    if has_s0:
        s0_ref = refs[pos]
        pos += 1
        pos += 1
    o_ref = refs[pos]
    pos += 1
    st_ref = None
    if want_state:
        st_ref = refs[pos]
        pos += 1
    lf_ref, kk_ref, acc_ref = refs[pos:pos + 3]

    ch = A_CHUNK
    nc = n // ch
    groups = ch // 8
    lb = lb_ref[...]
    for d, z_ref in ((0, zf_ref), (1, zb_ref)):
        z = z_ref[...]
        log_a, log_1m, one_m = lb[3 * d:3 * d + 1], lb[3 * d + 1:3 * d + 2], lb[3 * d + 2:3 * d + 3]
        log_sig = jnp.minimum(z, 0.0) - jnp.log1p(jnp.exp(-jnp.abs(z)))
        b = log_1m + log_sig
        lf_ref[d] = jnp.maximum(log_a, b) + jnp.log1p(jnp.exp(-jnp.abs(log_a - b)))
        kk_ref[d] = one_m * jax.nn.sigmoid(-z)

    r_io = lax.broadcasted_iota(jnp.int32, (ch, ch), 0)
    c_io = lax.broadcasted_iota(jnp.int32, (ch, ch), 1)
    sub_io = lax.broadcasted_iota(jnp.int32, (8, 1), 0)

    for d in range(2):
        tri = (c_io <= r_io if d == 0 else c_io >= r_io).astype(F32)

        def chunk(it, st_t, d=d, tri=tri):
            c = it if d == 0 else nc - 1 - it
            r0 = pl.multiple_of(c * ch, ch)
            qc = q_ref[pl.ds(r0, ch), :]
            kc = kk_ref[d, pl.ds(r0, ch), :]
            vc = v_ref[pl.ds(r0, ch), :]
            lfc = lf_ref[d, pl.ds(r0, ch), :]
            bc = jnp.dot(tri, lfc, precision=lax.Precision.HIGHEST, preferred_element_type=F32)
            btot = bc[ch - 1:ch, :] if d == 0 else bc[0:1, :]
            o = lax.dot_general((qc * jnp.exp(bc)).astype(BF16), st_t.astype(BF16),
                                (((1,), (1,)), ((), ())), preferred_element_type=F32)
            og = [o[8 * g:8 * g + 8, :] for g in range(groups)]
            for s in range(ch):
                gs = s // 8
                active = range(gs, groups) if d == 0 else range(0, gs + 1)
                bcs, ks, vs = bc[s:s + 1, :], kc[s:s + 1, :], vc[s:s + 1, :]
                for g in active:
                    dec = jnp.exp(jnp.minimum(bc[8 * g:8 * g + 8, :] - bcs, 0.0))
                    col = jnp.sum(qc[8 * g:8 * g + 8, :] * ks * dec, axis=1, keepdims=True)
                    if g == gs:
                        t_io = sub_io + 8 * g
                        col = jnp.where(t_io >= s if d == 0 else t_io <= s, col, 0.0)
                    og[g] = og[g] + col * vs
            o_all = jnp.concatenate(og, axis=0)
            if d == 0:
                acc_ref[pl.ds(r0, ch), :] = o_all
            else:
                acc_ref[pl.ds(r0, ch), :] += o_all
            kd = kc * jnp.exp(btot - bc)
            upd = lax.dot_general(vc.astype(BF16), kd.astype(BF16), (((0,), (0,)), ((), ())),
                                  preferred_element_type=F32)
            return st_t * jnp.exp(btot) + upd

        if has_s0:
            st0 = jnp.transpose(s0_ref[d])
        else:
            st0 = jnp.zeros((HD, HD), F32)
        st_fin = lax.fori_loop(0, nc, chunk, st0)
        if want_state:
            st_ref[d] = jnp.transpose(st_fin)

    gate = g_ref[...]
    o = _rms(acc_ref[...], ng_ref[...]) * (gate * jax.nn.sigmoid(gate))
    o_ref[...] = o.astype(BF16)


def _hgrn(proj, lbp, ng, s0, prev, n, nb, row0, j):
    has_s0 = s0 is not None
    want_state = not has_s0
    rb0 = row0 // n
    heads = A_HEADS

    def col(seg):
        return pl.BlockSpec((n, HD), lambda b, h: (rb0 + b, seg * heads + h))

    in_specs = [col(0), col(1), col(2), col(3), col(4),
                pl.BlockSpec((6, HD), lambda b, h: (0, h)),
                pl.BlockSpec((1, HD), lambda b, h: (0, 0))]
    args = [proj, proj, proj, proj, proj, lbp, ng.reshape(1, HD)]
    if has_s0:
        in_specs.append(pl.BlockSpec((None, None, 2, None, HD, HD), lambda b, h: (b, j, 0, h, 0, 0)))
        args.append(s0)
    aliases = {}
    if prev is not None:
        in_specs.append(pl.BlockSpec(memory_space=pl.ANY))
        args.append(prev)
        aliases = {len(args) - 1: 0}
    out_specs = [pl.BlockSpec((n, HD), lambda b, h: (rb0 + b, h))]
    out_shape = [jax.ShapeDtypeStruct((T_ALL, A_HEADS * HD), BF16)]
    if want_state:
        out_specs.append(pl.BlockSpec((None, 2, None, HD, HD), lambda b, h: (b, 0, h, 0, 0)))
        out_shape.append(jax.ShapeDtypeStruct((nb, 2, A_HEADS, HD, HD), F32))
    outs = pl.pallas_call(
        functools.partial(_hgrn_kernel, n=n, has_s0=has_s0, want_state=want_state),
        grid=(nb, heads),
        in_specs=in_specs,
        out_specs=out_specs,
        out_shape=out_shape,
        scratch_shapes=[pltpu.VMEM((2, n, HD), F32), pltpu.VMEM((2, n, HD), F32), pltpu.VMEM((n, HD), F32)],
        input_output_aliases=aliases,
        compiler_params=_cparams(("arbitrary", "arbitrary")),
        name="hgrn_lat" if has_s0 else "hgrn_ctx",
    )(*args)
    return (outs[0], outs[1]) if want_state else (outs[0], None)


def _rope(x, cos, sin):
    lane = lax.broadcasted_iota(jnp.int32, x.shape, 1)
    partner = jnp.where((lane & 32) == 0, pltpu.roll(x, 96, 1), pltpu.roll(x, 32, 1))
    return x * cos + partner * sin


def _attend(q, ks, vs, masks=None, sink=None):
    qb = q.astype(BF16)
    ss = [lax.dot_general(qb, k, (((1,), (1,)), ((), ())), preferred_element_type=F32) for k in ks]
    if masks is not None:
        ss = [s if mk is None else jnp.where(mk, s, MASK_VALUE) for s, mk in zip(ss, masks)]
    m = ss[0].max(axis=-1, keepdims=True)
    for s in ss[1:]:
        m = jnp.maximum(m, s.max(axis=-1, keepdims=True))
    if sink is not None:
        m = jnp.maximum(m, sink)
    den = jnp.exp(sink - m) if sink is not None else None
    acc = None
    for s, v in zip(ss, vs):
        p = jnp.exp(s - m)
        ps = p.sum(axis=-1, keepdims=True)
        den = ps if den is None else den + ps
        pv = jnp.dot(p.astype(BF16), v, preferred_element_type=F32)
        acc = pv if acc is None else acc + pv
    return acc / den


SCALE = HD ** -0.5
Q_ROWS = 256


def _attn_b_kernel(*refs, n, latent):
    if latent:
        q_ref, k_ref, v_ref, qg_ref, kg_ref, cos_ref, sin_ref, ck_ref, cv_ref, _, o_ref = refs
    else:
        q_ref, k_ref, v_ref, qg_ref, kg_ref, o_ref, ko_ref, vo_ref = refs
    k = _rms(k_ref[...], kg_ref[...])
    v = v_ref[...]
    if latent:
        k = _rope(k, cos_ref[...], sin_ref[...])
        ks = [k.astype(BF16), ck_ref[...].astype(BF16)]
        vs = [v.astype(BF16), cv_ref[...].astype(BF16)]
    else:
        ko_ref[...] = k
        vo_ref[...] = v
        ks, vs = [k.astype(BF16)], [v.astype(BF16)]
    per_kv = B_HEADS // B_KV
    for hh in range(per_kv):
        for r0 in range(0, n, Q_ROWS):
            q = _rms(q_ref[r0:r0 + Q_ROWS, hh * HD:(hh + 1) * HD], qg_ref[...])
            if latent:
                q = _rope(q, cos_ref[r0:r0 + Q_ROWS, :], sin_ref[r0:r0 + Q_ROWS, :])
            o = _attend(q * SCALE, ks, vs)
            o_ref[r0:r0 + Q_ROWS, hh * HD:(hh + 1) * HD] = o.astype(BF16)


def _attn_b(proj, qg, kg, j, latent, rope=None, cache_k=None, cache_v=None, prev=None):
    n, nb, row0 = (N_LAT, N_LAT_B, T_CTX) if latent else (N_CTX, N_CTX_B, 0)
    rb0 = row0 // n
    per_kv = B_HEADS // B_KV
    qcol0 = 5 * A_HEADS * HD // (per_kv * HD)
    kcol0 = (5 * A_HEADS + B_HEADS) * HD // HD
    in_specs = [
        pl.BlockSpec((n, per_kv * HD), lambda b, kv: (rb0 + b, qcol0 + kv)),
        pl.BlockSpec((n, HD), lambda b, kv: (rb0 + b, kcol0 + kv)),
        pl.BlockSpec((n, HD), lambda b, kv: (rb0 + b, kcol0 + B_KV + kv)),
        pl.BlockSpec((1, HD), lambda b, kv: (0, 0)),
        pl.BlockSpec((1, HD), lambda b, kv: (0, 0)),
    ]
    args = [proj, proj, proj, qg.reshape(1, HD), kg.reshape(1, HD)]
    o_spec = pl.BlockSpec((n, per_kv * HD), lambda b, kv: (rb0 + b, kv))
    o_shape = jax.ShapeDtypeStruct((T_ALL, B_HEADS * HD), BF16)
    if latent:
        tab = pl.BlockSpec((n, HD), lambda b, kv: (0, 0))
        cache = pl.BlockSpec((None, None, None, PAST, HD), lambda b, kv: (b, j, kv, 0, 0))
        in_specs += [tab, tab, cache, cache, pl.BlockSpec(memory_space=pl.ANY)]
        args += [rope[0], rope[1], cache_k, cache_v, prev]
        return pl.pallas_call(
            functools.partial(_attn_b_kernel, n=n, latent=True),
            grid=(nb, B_KV), in_specs=in_specs, out_specs=o_spec, out_shape=o_shape,
            input_output_aliases={len(args) - 1: 0},
            compiler_params=_cparams(("arbitrary", "arbitrary")), name="attn_b_lat",
        )(*args)
    kv_spec = pl.BlockSpec((None, None, n, HD), lambda b, kv: (b, kv, 0, 0))
    kv_shape = jax.ShapeDtypeStruct((nb, B_KV, n, HD), F32)
    return pl.pallas_call(
        functools.partial(_attn_b_kernel, n=n, latent=False),
        grid=(nb, B_KV), in_specs=in_specs, out_specs=[o_spec, kv_spec, kv_spec],
        out_shape=[o_shape, kv_shape, kv_shape],
        compiler_params=_cparams(("arbitrary", "arbitrary")), name="attn_b_ctx",
    )(*args)


def _attn_c_kernel(*refs, n, latent):
    if latent:
        q_ref, k_ref, v_ref, sink_ref, cos_ref, sin_ref, ck_ref, cv_ref, _, o_ref = refs
    else:
        q_ref, k_ref, v_ref, sink_ref, o_ref, ko_ref, vo_ref = refs
    kv = pl.program_id(1)
    k = k_ref[...]
    v = v_ref[...]
    if latent:
        k = _rope(k, cos_ref[...], sin_ref[...])
        ks = [k.astype(BF16), ck_ref[...].astype(BF16)]
        vs = [v.astype(BF16), cv_ref[...].astype(BF16)]
    else:
        ko_ref[...] = k
        vo_ref[...] = v
        ks, vs = [k.astype(BF16)], [v.astype(BF16)]
    per_kv = C_HEADS // C_KV
    lane_h = lax.broadcasted_iota(jnp.int32, (1, C_HEADS), 1)
    for hh in range(per_kv):
        sink = jnp.sum(jnp.where(lane_h == kv * per_kv + hh, sink_ref[...], 0.0), axis=1, keepdims=True)
        for r0 in range(0, n, Q_ROWS):
            q = q_ref[r0:r0 + Q_ROWS, hh * HD:(hh + 1) * HD]
            masks = None
            if latent:
                q = _rope(q, cos_ref[r0:r0 + Q_ROWS, :], sin_ref[r0:r0 + Q_ROWS, :])
                qp = lax.broadcasted_iota(jnp.int32, (Q_ROWS, n), 0) + r0
                kp = lax.broadcasted_iota(jnp.int32, (Q_ROWS, n), 1)
                masks = [jnp.abs(qp - kp) <= WINDOW, None]
            o = _attend(q * SCALE, ks, vs, masks=masks, sink=sink)
            o_ref[r0:r0 + Q_ROWS, hh * HD:(hh + 1) * HD] = o.astype(BF16)


def _attn_c(proj, sink, j, latent, rope=None, cache_k=None, cache_v=None, prev=None):
    n, nb, row0 = (N_LAT, N_LAT_B, T_CTX) if latent else (N_CTX, N_CTX_B, 0)
    rb0 = row0 // n
    per_kv = C_HEADS // C_KV
    kcol0 = C_HEADS
    in_specs = [
        pl.BlockSpec((n, per_kv * HD), lambda b, kv: (rb0 + b, kv)),
        pl.BlockSpec((n, HD), lambda b, kv: (rb0 + b, kcol0 + kv)),
        pl.BlockSpec((n, HD), lambda b, kv: (rb0 + b, kcol0 + C_KV + kv)),
        pl.BlockSpec((1, C_HEADS), lambda b, kv: (0, 0)),
    ]
    args = [proj, proj, proj, sink.reshape(1, C_HEADS)]
    o_spec = pl.BlockSpec((n, per_kv * HD), lambda b, kv: (rb0 + b, kv))
    o_shape = jax.ShapeDtypeStruct((T_ALL, C_HEADS * HD), BF16)
    if latent:
        tab = pl.BlockSpec((n, HD), lambda b, kv: (0, 0))
        cache = pl.BlockSpec((None, None, None, PAST, HD), lambda b, kv: (b, j, kv, 0, 0))
        in_specs += [tab, tab, cache, cache, pl.BlockSpec(memory_space=pl.ANY)]
        args += [rope[0], rope[1], cache_k, cache_v, prev]
        return pl.pallas_call(
            functools.partial(_attn_c_kernel, n=n, latent=True),
            grid=(nb, C_KV), in_specs=in_specs, out_specs=o_spec, out_shape=o_shape,
            input_output_aliases={len(args) - 1: 0},
            compiler_params=_cparams(("arbitrary", "arbitrary")), name="attn_c_lat",
        )(*args)
    kv_spec = pl.BlockSpec((None, None, n, HD), lambda b, kv: (b, kv, 0, 0))
    kv_shape = jax.ShapeDtypeStruct((nb, C_KV, n, HD), F32)
    return pl.pallas_call(
        functools.partial(_attn_c_kernel, n=n, latent=False),
        grid=(nb, C_KV), in_specs=in_specs, out_specs=[o_spec, kv_spec, kv_spec],
        out_shape=[o_shape, kv_shape, kv_shape],
        compiler_params=_cparams(("arbitrary", "arbitrary")), name="attn_c_ctx",
    )(*args)


def _attn_d_kernel(*refs, n, latent, lam_init):
    if latent:
        q_ref, k_ref, v_ref, lam_ref, sg_ref, cos_ref, sin_ref, ck_ref, cv_ref, _, o_ref = refs
    else:
        q_ref, k_ref, v_ref, lam_ref, sg_ref, o_ref, ko_ref, vo_ref = refs
    lv = lam_ref[...]
    dot01 = jnp.sum(jnp.sum(lv[0:1, :] * lv[1:2, :], axis=1, keepdims=True), axis=0, keepdims=True)
    dot23 = jnp.sum(jnp.sum(lv[2:3, :] * lv[3:4, :], axis=1, keepdims=True), axis=0, keepdims=True)
    lam = jnp.exp(dot01) - jnp.exp(dot23) + lam_init
    v = v_ref[...]
    kss = []
    for comp in range(2):
        k = k_ref[:, comp * HD:(comp + 1) * HD]
        if latent:
            k = _rope(k, cos_ref[...], sin_ref[...])
            kss.append([k.astype(BF16), ck_ref[comp].astype(BF16)])
        else:
            ko_ref[comp] = k
            kss.append([k.astype(BF16)])
    if latent:
        vs = [v.astype(BF16), cv_ref[...].astype(BF16)]
    else:
        vo_ref[...] = v
        vs = [v.astype(BF16)]
    for r0 in range(0, n, Q_ROWS):
        os_ = []
        for comp in range(2):
            q = q_ref[r0:r0 + Q_ROWS, comp * HD:(comp + 1) * HD]
            if latent:
                q = _rope(q, cos_ref[r0:r0 + Q_ROWS, :], sin_ref[r0:r0 + Q_ROWS, :])
            os_.append(_attend(q * SCALE, kss[comp], vs))
        o = _rms(os_[0] - lam * os_[1], sg_ref[...]) * (1.0 - lam_init)
        o_ref[r0:r0 + Q_ROWS, :] = o.astype(BF16)


def _attn_d(proj, lam_vecs, subln_g, lam_init, j, latent, rope=None, cache_k=None, cache_v=None, prev=None):
    n, nb, row0 = (N_LAT, N_LAT_B, T_CTX) if latent else (N_CTX, N_CTX_B, 0)
    rb0 = row0 // n
    w2 = 2 * HD
    qcol0 = (C_HEADS + 2 * C_KV) * HD // w2
    kcol0 = qcol0 + D_HEADS
    vcol0 = kcol0 + D_HEADS
    in_specs = [
        pl.BlockSpec((n, w2), lambda b, h: (rb0 + b, qcol0 + h)),
        pl.BlockSpec((n, w2), lambda b, h: (rb0 + b, kcol0 + h)),
        pl.BlockSpec((n, w2), lambda b, h: (rb0 + b, vcol0 + h)),
        pl.BlockSpec((4, HD), lambda b, h: (0, 0)),
        pl.BlockSpec((1, w2), lambda b, h: (0, 0)),
    ]
    args = [proj, proj, proj, lam_vecs, subln_g.reshape(1, w2)]
    o_spec = pl.BlockSpec((n, w2), lambda b, h: (rb0 + b, h))
    o_shape = jax.ShapeDtypeStruct((T_ALL, D_HEADS * w2), BF16)
    kern = functools.partial(_attn_d_kernel, n=n, latent=latent, lam_init=lam_init)
    if latent:
        tab = pl.BlockSpec((n, HD), lambda b, h: (0, 0))
        in_specs += [tab, tab,
                     pl.BlockSpec((None, None, None, 2, PAST, HD), lambda b, h: (b, j, h, 0, 0, 0)),
                     pl.BlockSpec((None, None, None, PAST, w2), lambda b, h: (b, j, h, 0, 0)),
                     pl.BlockSpec(memory_space=pl.ANY)]
        args += [rope[0], rope[1], cache_k, cache_v, prev]
        return pl.pallas_call(
            kern, grid=(nb, D_HEADS), in_specs=in_specs, out_specs=o_spec, out_shape=o_shape,
            input_output_aliases={len(args) - 1: 0},
            compiler_params=_cparams(("arbitrary", "arbitrary")), name="attn_d_lat",
        )(*args)
    k_spec = pl.BlockSpec((None, None, 2, n, HD), lambda b, h: (b, h, 0, 0, 0))
    v_spec = pl.BlockSpec((None, None, n, w2), lambda b, h: (b, h, 0, 0))
    return pl.pallas_call(
        kern, grid=(nb, D_HEADS), in_specs=in_specs, out_specs=[o_spec, k_spec, v_spec],
        out_shape=[o_shape, jax.ShapeDtypeStruct((nb, D_HEADS, 2, n, HD), F32),
                   jax.ShapeDtypeStruct((nb, D_HEADS, n, w2), F32)],
        compiler_params=_cparams(("arbitrary", "arbitrary")), name="attn_d_ctx",
    )(*args)


def _rope_tables():
    rows = N_LAT // GRID_W
    row_ids = jnp.repeat(jnp.arange(rows), GRID_W).astype(F32)
    col_ids = jnp.tile(jnp.arange(GRID_W), rows).astype(F32)
    half = HD // 2
    inv_freq = 1.0 / (ROPE_THETA ** (jnp.arange(0, half, 2, dtype=F32) / half))
    ar, ac = row_ids[:, None] * inv_freq, col_ids[:, None] * inv_freq
    cos = jnp.concatenate([jnp.cos(ar), jnp.cos(ar), jnp.cos(ac), jnp.cos(ac)], axis=-1)
    sin = jnp.concatenate([-jnp.sin(ar), jnp.sin(ar), -jnp.sin(ac), jnp.sin(ac)], axis=-1)
    return cos, sin


def _hgrn_bounds(logits):
    lb_p = jax.nn.softmax(logits.astype(F32), axis=1)
    lb = jnp.cumsum(lb_p, axis=1) - lb_p[:, :1]
    rows = jnp.stack([jnp.log(jnp.maximum(lb, LB_FLOOR)), jnp.log1p(-lb), 1.0 - lb], axis=1)
    return jnp.transpose(rows, (2, 0, 1, 3)).reshape(logits.shape[1], 6, logits.shape[2])


def kernel(x_prompt, x_sample, state_hgrn, cache_b_k, cache_b_v, cache_c_k, cache_c_v, cache_d_k, cache_d_v, c, c_ctx, norm_g, w_ada, b_ada, w_in_even, w_out_even, hgrn_lb_logits, hgrn_norm_g, b_qnorm_g, b_knorm_g, w_in_odd, w_out_odd, c_sink, d_lambda, d_subln_g, peer_wq, peer_subkeys, peer_u, peer_v, final_norm_g):
    x = jnp.concatenate([x_prompt.reshape(T_CTX, D_MODEL), x_sample.reshape(T_LAT, D_MODEL)], axis=0)
    cond = jnp.concatenate([c_ctx[None, :], c, jnp.zeros((N_MOD_ROWS - 1 - N_LAT_B, D_MODEL), F32)], axis=0)
    m4 = _ada(cond, w_ada, b_ada).reshape(DEPTH, N_MOD_ROWS, 1, 6 * D_MODEL)
    rope = _rope_tables()
    lbp = _hgrn_bounds(hgrn_lb_logits)

    states, bks, bvs, cks, cvs, dks, dvs = [], [], [], [], [], [], []
    y = None
    for l in range(DEPTH):
        j = l // 2
        if l % 2 == 0:
            proj = _pre_in(x, norm_g[l, 0], m4, l, w_in_even[j].astype(BF16))
            mix_a, st = _hgrn(proj, lbp[j], hgrn_norm_g[j], None, None, N_CTX, N_CTX_B, 0, j)
            mix_a, _ = _hgrn(proj, lbp[j], hgrn_norm_g[j], state_hgrn, mix_a, N_LAT, N_LAT_B, T_CTX, j)
            mix_b, bk, bv = _attn_b(proj, b_qnorm_g[j], b_knorm_g[j], j, False)
            mix_b = _attn_b(proj, b_qnorm_g[j], b_knorm_g[j], j, True, rope, cache_b_k, cache_b_v, mix_b)
            states.append(st)
            bks.append(bk)
            bvs.append(bv)
            w_out = w_out_even[j]
        else:
            lam_init = 0.8 - 0.6 * math.exp(-0.3 * l)
            proj = _pre_in(x, norm_g[l, 0], m4, l, w_in_odd[j].astype(BF16))
            mix_a, ck, cv = _attn_c(proj, c_sink[j], j, False)
            mix_a = _attn_c(proj, c_sink[j], j, True, rope, cache_c_k, cache_c_v, mix_a)
            mix_b, dk, dv = _attn_d(proj, d_lambda[j], d_subln_g[j], lam_init, j, False)
            mix_b = _attn_d(proj, d_lambda[j], d_subln_g[j], lam_init, j, True, rope, cache_d_k, cache_d_v, mix_b)
            cks.append(ck)
            cvs.append(cv)
            dks.append(dk)
            dvs.append(dv)
            w_out = w_out_odd[j]
        x1, ht = _post_out(mix_a, mix_b, w_out.astype(BF16), x, m4, l, norm_g[l, 1])
        wqt = jnp.transpose(peer_wq[l]).astype(BF16)
        sk = peer_subkeys[l].reshape(2 * P_HEADS, P_KEYS, P_KEYS).astype(BF16)
        s1, s2, e1, e2, tau = _peer_scores(ht, wqt, sk)
        ot = _peer_main(ht, peer_u[l].astype(BF16), jnp.transpose(peer_v[l]).astype(BF16), s1, s2, e1, e2, tau)
        x, y = _peer_post(ot, x1, m4, l, final_norm_g if l == DEPTH - 1 else None)

    y_prompt = y[:T_CTX].reshape(N_CTX_B, N_CTX, D_MODEL)
    y_sample = y[T_CTX:].reshape(N_LAT_B, N_LAT, D_MODEL)
    return (y_prompt, y_sample, jnp.stack(states, axis=1), jnp.stack(bks, axis=1), jnp.stack(bvs, axis=1),
            jnp.stack(cks, axis=1), jnp.stack(cvs, axis=1), jnp.stack(dks, axis=1), jnp.stack(dvs, axis=1))
```
